```python
import math
import jax, jax.numpy as jnp
from jax import lax
import numpy as np

D_MODEL = 2048
BATCH = 8
SEQ = 4096
DEPTH = 1

CHUNK = 64
GMLP_WIDTH = D_MODEL // 2
GMLP_GROUPS = 8
GMLP_GROUP_DIM = GMLP_WIDTH // GMLP_GROUPS
GMLP_BLOCK = 128
POOL_WIDTH = D_MODEL // 2
POOL_WINDOWS = (2, 4, 8, 16)
POOL_GROUPS = len(POOL_WINDOWS)
POOL_GROUP_DIM = POOL_WIDTH // POOL_GROUPS
SPLIT_U = GMLP_WIDTH
SPLIT_V = 2 * GMLP_WIDTH
SPLIT_P = 2 * GMLP_WIDTH + POOL_WIDTH
SPLIT_GA = SPLIT_P + D_MODEL
PROJ_COLS = SPLIT_GA + D_MODEL
N_EXPERTS = 256
TOP_K = 8
N_GROUPS = 8
TOPK_GROUPS = 4
D_EXPERT = D_MODEL // 4
ROUTED_SCALE = 2.5
EXPERT_BLOCK = 128
LN_EPS = 1e-5
DEEPNORM_ALPHA = (2.0 * DEPTH) ** 0.25
DEEPNORM_BETA = (8.0 * DEPTH) ** -0.25

kernel_name = "hybrid_gmlp_pool_moe_deepnorm_adaln"


def layer_norm(x, gain, bias):
    xf = x.astype(jnp.float32)
    mu = jnp.mean(xf, axis=-1, keepdims=True)
    var = jnp.mean(jnp.square(xf - mu), axis=-1, keepdims=True)
    y = (xf - mu) * lax.rsqrt(var + LN_EPS)
    return (y * gain.astype(jnp.float32) + bias.astype(jnp.float32)).astype(x.dtype)


def gmlp_spatial_gating(u, v, w_s, b_s, g_v, b_v):
    bsz, seq, _ = v.shape
    nb = seq // GMLP_BLOCK
    vg = v.reshape(bsz, seq, GMLP_GROUPS, GMLP_GROUP_DIM)
    vg = layer_norm(vg, g_v.reshape(GMLP_GROUPS, GMLP_GROUP_DIM), b_v.reshape(GMLP_GROUPS, GMLP_GROUP_DIM))
    vb = vg.reshape(bsz, nb, GMLP_BLOCK, GMLP_GROUPS, GMLP_GROUP_DIM)
    pos = jnp.arange(GMLP_BLOCK)
    mask = (pos[None, :] // CHUNK) <= (pos[:, None] // CHUNK)
    w = jnp.where(mask[None], w_s, jnp.zeros((), w_s.dtype))
    s = jnp.einsum('gij,bnjgc->bnigc', w, vb) + jnp.swapaxes(b_s, 0, 1)[:, :, None]
    return u * s.reshape(bsz, seq, GMLP_WIDTH)


def multiscale_pool(p, w_pool, b_pool, ls):
    bsz, seq, _ = p.shape
    pg = p.reshape(bsz, seq, POOL_GROUPS, POOL_GROUP_DIM)
    t = jnp.arange(1, seq + 1, dtype=jnp.float32)
    outs = []
    for g, win in enumerate(POOL_WINDOWS):
        xg = pg[:, :, g, :].astype(jnp.float32)
        cs = jnp.cumsum(xg, axis=1)
        lag = jnp.pad(cs, ((0, 0), (win, 0), (0, 0)))[:, :seq]
        cnt = jnp.minimum(t, float(win))[None, :, None]
        outs.append((cs - lag) / cnt - xg)
    pooled = jnp.stack(outs, axis=2).astype(p.dtype)
    y = jnp.einsum('bsgc,gcd->bsgd', pooled, w_pool) + b_pool.reshape(POOL_GROUPS, POOL_GROUP_DIM)
    return y.reshape(bsz, seq, POOL_WIDTH) * ls


def swiglu(h, wg, wu, wd):
    return (jax.nn.silu(h @ wg) * (h @ wu)) @ wd


def route(h, w_router, b_router):
    n = h.shape[0]
    scores = jax.nn.sigmoid(h.astype(jnp.float32) @ w_router.astype(jnp.float32))
    sel = scores + b_router.astype(jnp.float32)
    gs = sel.reshape(n, N_GROUPS, N_EXPERTS // N_GROUPS)
    group_score = jnp.sum(lax.top_k(gs, 2)[0], axis=-1)
    _, gidx = lax.top_k(group_score, TOPK_GROUPS)
    gmask = jnp.any(gidx[:, :, None] == jnp.arange(N_GROUPS)[None, None, :], axis=1)
    emask = jnp.repeat(gmask, N_EXPERTS // N_GROUPS, axis=1)
    masked = jnp.where(emask, sel, -jnp.inf)
    _, eidx = lax.top_k(masked, TOP_K)
    ew = jnp.take_along_axis(scores, eidx, axis=1)
    ew = ew / jnp.sum(ew, axis=-1, keepdims=True) * ROUTED_SCALE
    return eidx.astype(jnp.int32), ew


def routed_experts(h, eidx, ew, w_gate, w_up, w_down):
    n, d = h.shape
    a = n * TOP_K
    n_blocks = -(-(a + N_EXPERTS * (EXPERT_BLOCK - 1)) // EXPERT_BLOCK)
    p_rows = n_blocks * EXPERT_BLOCK
    flat_e = eidx.reshape(a)
    flat_w = ew.reshape(a)
    order = jnp.argsort(flat_e)
    sorted_e = flat_e[order]
    tok = (order // TOP_K).astype(jnp.int32)
    counts = jnp.bincount(flat_e, length=N_EXPERTS).astype(jnp.int32)
    starts = jnp.cumsum(counts) - counts
    padded = (counts + EXPERT_BLOCK - 1) // EXPERT_BLOCK * EXPERT_BLOCK
    pends = jnp.cumsum(padded)
    pstarts = pends - padded
    dest = pstarts[sorted_e] + (jnp.arange(a, dtype=jnp.int32) - starts[sorted_e])
    row_tok = jnp.full((p_rows,), n, jnp.int32).at[dest].set(tok)
    row_w = jnp.zeros((p_rows,), jnp.float32).at[dest].set(flat_w[order])
    block_start = jnp.arange(n_blocks, dtype=jnp.int32) * EXPERT_BLOCK
    block_e = jnp.minimum(jnp.searchsorted(pends, block_start, side='right'), N_EXPERTS - 1)
    h_pad = jnp.concatenate([h, jnp.zeros((1, d), h.dtype)], axis=0)

    def body(acc, blk):
        e, t, w = blk
        y = swiglu(h_pad[t], w_gate[e], w_up[e], w_down[e]).astype(jnp.float32) * w[:, None]
        return acc.at[t].add(y), None

    acc0 = jnp.zeros((n + 1, d), jnp.float32)
    acc, _ = lax.scan(body, acc0, (block_e.astype(jnp.int32),
                                   row_tok.reshape(n_blocks, EXPERT_BLOCK),
                                   row_w.reshape(n_blocks, EXPERT_BLOCK)))
    return acc[:n].astype(h.dtype)


def setup_inputs(seed: int = 0) -> dict:
    key = jax.random.key(seed)
    ks = jax.random.split(key, 32)
    nrm = lambda k, shape, s: jax.random.normal(k, shape, jnp.float32) * s
    D, L = D_MODEL, DEPTH
    return {
        "x": nrm(ks[0], (BATCH, SEQ, D), 1.0),
        "c": nrm(ks[1], (BATCH, D), 1.0),
        "w_ada": nrm(ks[2], (L, D, 6 * D), 0.5 * D ** -0.5),
        "b_ada": nrm(ks[3], (L, 6 * D), 0.01),
        "w_in": nrm(ks[4], (L, D, PROJ_COLS), D ** -0.5),
        "w_s": nrm(ks[5], (L, GMLP_GROUPS, GMLP_BLOCK, GMLP_BLOCK), GMLP_BLOCK ** -0.5),
        "b_s": 1.0 + nrm(ks[6], (L, GMLP_GROUPS, GMLP_BLOCK), 0.1),
        "g_v": 1.0 + nrm(ks[7], (L, GMLP_WIDTH), 0.01),
        "b_v": nrm(ks[8], (L, GMLP_WIDTH), 0.01),
        "w_pool": nrm(ks[9], (L, POOL_GROUPS, POOL_GROUP_DIM, POOL_GROUP_DIM), POOL_GROUP_DIM ** -0.5),
        "b_pool": nrm(ks[10], (L, POOL_WIDTH), 0.01),
        "ls_pool": 1.0 + nrm(ks[11], (L, POOL_WIDTH), 0.1),
        "w_pa": nrm(ks[12], (L, GMLP_WIDTH, D), GMLP_WIDTH ** -0.5),
        "w_pb": nrm(ks[13], (L, POOL_WIDTH, D), POOL_WIDTH ** -0.5),
        "w_o": nrm(ks[14], (L, D, D), DEEPNORM_BETA * D ** -0.5),
        "ln1_g": 1.0 + nrm(ks[15], (L, D), 0.01),
        "ln1_b": nrm(ks[16], (L, D), 0.01),
        "w_router": nrm(ks[17], (L, D, N_EXPERTS), D ** -0.5),
        "b_router": nrm(ks[18], (L, N_EXPERTS), 0.01),
        "w_gate": nrm(ks[19], (L, N_EXPERTS, D, D_EXPERT), D ** -0.5),
        "w_up": nrm(ks[20], (L, N_EXPERTS, D, D_EXPERT), D ** -0.5),
        "w_down": nrm(ks[21], (L, N_EXPERTS, D_EXPERT, D), DEEPNORM_BETA * D_EXPERT ** -0.5),
        "ws_gate": nrm(ks[22], (L, D, D_EXPERT), D ** -0.5),
        "ws_up": nrm(ks[23], (L, D, D_EXPERT), D ** -0.5),
        "ws_down": nrm(ks[24], (L, D_EXPERT, D), DEEPNORM_BETA * D_EXPERT ** -0.5),
        "ln2_g": 1.0 + nrm(ks[25], (L, D), 0.01),
        "ln2_b": nrm(ks[26], (L, D), 0.01),
    }


def reference(x, c, w_ada, b_ada, w_in, w_s, b_s, g_v, b_v, w_pool, b_pool, ls_pool,
              w_pa, w_pb, w_o, ln1_g, ln1_b, w_router, b_router, w_gate, w_up, w_down,
              ws_gate, ws_up, ws_down, ln2_g, ln2_b):
    bsz, seq, d = x.shape
    for l in range(DEPTH):
        mod = (jax.nn.silu(c) @ w_ada[l] + b_ada[l])[:, None, :]
        sh1, sc1, g1, sh2, sc2, g2 = jnp.split(mod, 6, axis=-1)

        h = x * (1.0 + sc1) + sh1
        z = h @ w_in[l]
        zu, zv, zp, zga, zgb = jnp.split(z, [SPLIT_U, SPLIT_V, SPLIT_P, SPLIT_GA], axis=-1)
        y_a = gmlp_spatial_gating(jax.nn.gelu(zu, approximate=False), jax.nn.gelu(zv, approximate=False),
                                  w_s[l], b_s[l], g_v[l], b_v[l])
        y_b = multiscale_pool(zp, w_pool[l], b_pool[l], ls_pool[l])
        merged = jax.nn.sigmoid(zga) * (y_a @ w_pa[l]) + jax.nn.sigmoid(zgb) * (y_b @ w_pb[l])
        mix = merged @ w_o[l]
        x = layer_norm(DEEPNORM_ALPHA * x + g1 * mix, ln1_g[l], ln1_b[l])

        h = (x * (1.0 + sc2) + sh2).reshape(bsz * seq, d)
        eidx, ew = route(h, w_router[l], b_router[l])
        y = routed_experts(h, eidx, ew, w_gate[l], w_up[l], w_down[l]) + swiglu(h, ws_gate[l], ws_up[l], ws_down[l])
        x = layer_norm(DEEPNORM_ALPHA * x + g2 * y.reshape(bsz, seq, d), ln2_g[l], ln2_b[l])
    return x
```

```python
import functools
import math

import jax
import jax.numpy as jnp
from jax import lax
from jax.experimental import pallas as pl
from jax.experimental.pallas import tpu as pltpu

F32 = jnp.float32
BF16 = jnp.bfloat16

CHUNK = 64
GMLP_GROUPS = 8
GMLP_BLOCK = 128
POOL_WINDOWS = (2, 4, 8, 16)
TOP_K = 8
N_GROUPS = 8
TOPK_GROUPS = 4
ROUTED_SCALE = 2.5
LN_EPS = 1e-5

V7X_LANES = 128
V7X_SUBLANES = 8
V7X_VMEM_LIMIT_BYTES = 56 * 1024 * 1024

TOKEN_TILE = 256
ROUTER_TILE = 512
EXPERT_TILE = 256
POOL_HALO = 16


def _resident(shape, index_map):
    return pl.BlockSpec(shape, index_map, pipeline_mode=pl.Buffered(1))


def _modulate(x, scale, shift):
    return x * (1.0 + scale) + shift


def _layer_norm(r, gain, bias):
    mu = jnp.mean(r, axis=-1, keepdims=True)
    d = r - mu
    var = jnp.mean(d * d, axis=-1, keepdims=True)
    return d * lax.rsqrt(var + LN_EPS) * gain + bias


def _gelu(x):
    return 0.5 * x * (1.0 + lax.erf(x * math.sqrt(0.5)))


def _silu(x):
    return x * jax.nn.sigmoid(x)


def _bdot(a, b):
    return jnp.dot(a, b, preferred_element_type=F32)


def _mod_kernel(c_ref, w_ref, b_ref, o_ref):
    s = _silu(c_ref[...])
    o_ref[...] = jnp.dot(s, w_ref[...], precision=lax.Precision.HIGHEST,
                         preferred_element_type=F32) + b_ref[...]


def _adaln_mod(c, w_ada, b_ada):
    bsz, d = c.shape
    n = w_ada.shape[1]
    tn = 1024
    return pl.pallas_call(
        _mod_kernel,
        grid=(n // tn,),
        in_specs=[pl.BlockSpec((bsz, d), lambda j: (0, 0)),
                  pl.BlockSpec((d, tn), lambda j: (0, j)),
                  pl.BlockSpec((1, tn), lambda j: (0, j))],
        out_specs=pl.BlockSpec((bsz, tn), lambda j: (0, j)),
        out_shape=jax.ShapeDtypeStruct((bsz, n), F32),
        name="adaln_mod",
    )(c, w_ada, b_ada.reshape(1, n))


def _gmlp_kernel(x_ref, sc_ref, sh_ref, wuv_ref, wga_ref, wpa_ref, ws_ref, bst_ref,
                 gv_ref, bv_ref, o_ref, ya_ref):
    tm = x_ref.shape[1]
    width = wpa_ref.shape[0]
    gdim = width // GMLP_GROUPS
    h = _modulate(x_ref[0], sc_ref[0], sh_ref[0]).astype(BF16)
    zuv = _bdot(h, wuv_ref[...])
    u = _gelu(zuv[:, :width])
    v = _gelu(zuv[:, width:])
    qi = lax.broadcasted_iota(jnp.int32, (GMLP_BLOCK, GMLP_BLOCK), 0) // CHUNK
    kj = lax.broadcasted_iota(jnp.int32, (GMLP_BLOCK, GMLP_BLOCK), 1) // CHUNK
    visible = kj <= qi
    for g in range(GMLP_GROUPS):
        lo, hi = g * gdim, (g + 1) * gdim
        vn = _layer_norm(v[:, lo:hi], gv_ref[:, lo:hi], bv_ref[:, lo:hi]).astype(BF16)
        wg = jnp.where(visible, ws_ref[g], 0.0).astype(BF16)
        for blk in range(tm // GMLP_BLOCK):
            r0, r1 = blk * GMLP_BLOCK, (blk + 1) * GMLP_BLOCK
            s = _bdot(wg, vn[r0:r1]) + bst_ref[:, g:g + 1]
            ya_ref[r0:r1, lo:hi] = (u[r0:r1, lo:hi] * s).astype(BF16)
    pa = _bdot(ya_ref[...], wpa_ref[...])
    zga = _bdot(h, wga_ref[...])
    o_ref[0] = (jax.nn.sigmoid(zga) * pa).astype(o_ref.dtype)


def _gmlp_branch(x, sc1, sh1, w_uv, w_ga, w_pa, w_s, b_s_t, g_v, b_v):
    bsz, seq, d = x.shape
    tm = TOKEN_TILE
    width = w_pa.shape[0]
    const2 = lambda b, j: (0, 0)
    return pl.pallas_call(
        _gmlp_kernel,
        grid=(bsz, seq // tm),
        in_specs=[pl.BlockSpec((1, tm, d), lambda b, j: (b, j, 0)),
                  pl.BlockSpec((1, 1, d), lambda b, j: (b, 0, 0)),
                  pl.BlockSpec((1, 1, d), lambda b, j: (b, 0, 0)),
                  _resident(w_uv.shape, const2),
                  _resident(w_ga.shape, const2),
                  _resident(w_pa.shape, const2),
                  _resident(w_s.shape, lambda b, j: (0, 0, 0)),
                  _resident(b_s_t.shape, const2),
                  _resident(g_v.shape, const2),
                  _resident(b_v.shape, const2)],
        out_specs=pl.BlockSpec((1, tm, d), lambda b, j: (b, j, 0)),
        out_shape=jax.ShapeDtypeStruct((bsz, seq, d), BF16),
        scratch_shapes=[pltpu.VMEM((tm, width), BF16)],
        compiler_params=pltpu.CompilerParams(
            dimension_semantics=("parallel", "parallel"),
            vmem_limit_bytes=V7X_VMEM_LIMIT_BYTES),
        name="gmlp_branch",
    )(x, sc1, sh1, w_uv, w_ga, w_pa, w_s, b_s_t, g_v, b_v)


def _pool_kernel(x_ref, sc_ref, sh_ref, wp_ref, wgb_ref, wpb_ref, wpool_ref, bpool_ref,
                 ls_ref, o_ref, zbuf, yb_ref):
    j = pl.program_id(1)
    tm = x_ref.shape[1]
    width = wpb_ref.shape[0]
    gdim = width // len(POOL_WINDOWS)
    h = _modulate(x_ref[0], sc_ref[0], sh_ref[0]).astype(BF16)

    @pl.when(j == 0)
    def _():
        zbuf[0:POOL_HALO, :] = jnp.zeros((POOL_HALO, width), F32)

    zbuf[POOL_HALO:POOL_HALO + tm, :] = _bdot(h, wp_ref[...])
    t = lax.broadcasted_iota(jnp.int32, (tm, 1), 0) + j * tm
    for g, win in enumerate(POOL_WINDOWS):
        lo, hi = g * gdim, (g + 1) * gdim
        cur = zbuf[POOL_HALO:POOL_HALO + tm, lo:hi]
        acc = cur
        for k in range(1, win):
            acc = acc + zbuf[POOL_HALO - k:POOL_HALO - k + tm, lo:hi]
        cnt = jnp.minimum(t + 1, win).astype(F32)
        pooled = acc / cnt - cur
        y = _bdot(pooled.astype(BF16), wpool_ref[g]) + bpool_ref[:, lo:hi]
        yb_ref[:, lo:hi] = (y * ls_ref[:, lo:hi]).astype(BF16)
    zbuf[0:POOL_HALO, :] = zbuf[tm:tm + POOL_HALO, :]
    pb = _bdot(yb_ref[...], wpb_ref[...])
    zgb = _bdot(h, wgb_ref[...])
    o_ref[0] = (jax.nn.sigmoid(zgb) * pb).astype(o_ref.dtype)


def _pool_branch(x, sc1, sh1, w_p, w_gb, w_pb, w_pool, b_pool, ls_pool):
    bsz, seq, d = x.shape
    tm = TOKEN_TILE
    width = w_pb.shape[0]
    const2 = lambda b, j: (0, 0)
    return pl.pallas_call(
        _pool_kernel,
        grid=(bsz, seq // tm),
        in_specs=[pl.BlockSpec((1, tm, d), lambda b, j: (b, j, 0)),
                  pl.BlockSpec((1, 1, d), lambda b, j: (b, 0, 0)),
                  pl.BlockSpec((1, 1, d), lambda b, j: (b, 0, 0)),
                  _resident(w_p.shape, const2),
                  _resident(w_gb.shape, const2),
                  _resident(w_pb.shape, const2),
                  _resident(w_pool.shape, lambda b, j: (0, 0, 0)),
                  _resident(b_pool.shape, const2),
                  _resident(ls_pool.shape, const2)],
        out_specs=pl.BlockSpec((1, tm, d), lambda b, j: (b, j, 0)),
        out_shape=jax.ShapeDtypeStruct((bsz, seq, d), BF16),
        scratch_shapes=[pltpu.VMEM((POOL_HALO + tm, width), F32),
                        pltpu.VMEM((tm, width), BF16)],
        compiler_params=pltpu.CompilerParams(
            dimension_semantics=("arbitrary", "arbitrary"),
            vmem_limit_bytes=V7X_VMEM_LIMIT_BYTES),
        name="pool_branch",
    )(x, sc1, sh1, w_p, w_gb, w_pb, w_pool, b_pool, ls_pool)


def _post_mix_kernel(alpha, ma_ref, mb_ref, x_ref, g1_ref, sc2_ref, sh2_ref, wo_ref,
                     lng_ref, lnb_ref, x1_ref, h2_ref):
    m = (ma_ref[0].astype(F32) + mb_ref[0].astype(F32)).astype(BF16)
    mix = _bdot(m, wo_ref[...])
    x1 = _layer_norm(alpha * x_ref[0] + g1_ref[0] * mix, lng_ref[...], lnb_ref[...])
    x1_ref[0] = x1
    h2_ref[0] = _modulate(x1, sc2_ref[0], sh2_ref[0])


def _post_mix(alpha, m_a, m_b, x, g1, sc2, sh2, w_o, ln_g, ln_b):
    bsz, seq, d = x.shape
    tm = TOKEN_TILE
    tile = pl.BlockSpec((1, tm, d), lambda b, j: (b, j, 0))
    vec = pl.BlockSpec((1, 1, d), lambda b, j: (b, 0, 0))
    const2 = lambda b, j: (0, 0)
    return pl.pallas_call(
        functools.partial(_post_mix_kernel, alpha),
        grid=(bsz, seq // tm),
        in_specs=[tile, tile, tile, vec, vec, vec,
                  _resident(w_o.shape, const2),
                  _resident(ln_g.shape, const2),
                  _resident(ln_b.shape, const2)],
        out_specs=[tile, tile],
        out_shape=[jax.ShapeDtypeStruct((bsz, seq, d), F32),
                   jax.ShapeDtypeStruct((bsz, seq, d), F32)],
        compiler_params=pltpu.CompilerParams(
            dimension_semantics=("parallel", "parallel"),
            vmem_limit_bytes=V7X_VMEM_LIMIT_BYTES),
        name="post_mix",
    )(m_a, m_b, x, g1, sc2, sh2, w_o, ln_g, ln_b)


def _router_kernel(h_ref, whi_ref, wlo_ref, br_ref, eidx_ref, ew_ref, pos_ref, cnt_ref,
                   carry_ref):
    i = pl.program_id(0)

    @pl.when(i == 0)
    def _():
        carry_ref[...] = jnp.zeros(carry_ref.shape, F32)

    h = h_ref[...]
    h_hi = h.astype(BF16)
    h_lo = (h - h_hi.astype(F32)).astype(BF16)
    nt = (((1,), (1,)), ((), ()))
    w_hi = whi_ref[...]
    logits = (lax.dot_general(w_hi, h_hi, nt, preferred_element_type=F32)
              + lax.dot_general(w_hi, h_lo, nt, preferred_element_type=F32)
              + lax.dot_general(wlo_ref[...], h_hi, nt, preferred_element_type=F32))
    scores = jax.nn.sigmoid(logits)
    sel = scores + br_ref[...]
    n_exp, tm = sel.shape
    gsz = n_exp // N_GROUPS
    neg = -jnp.inf

    gs = []
    sub = lax.broadcasted_iota(jnp.int32, (gsz, tm), 0)
    for g in range(N_GROUPS):
        blk = sel[g * gsz:(g + 1) * gsz]
        m1 = jnp.max(blk, axis=0, keepdims=True)
        first = jnp.min(jnp.where(blk == m1, sub, gsz), axis=0, keepdims=True)
        m2 = jnp.max(jnp.where(sub == first, neg, blk), axis=0, keepdims=True)
        gs.append(m1 + m2)
    pieces = []
    for g in range(N_GROUPS):
        beaten = jnp.zeros((1, tm), jnp.int32)
        for o in range(N_GROUPS):
            if o == g:
                continue
            better = (gs[o] >= gs[g]) if o < g else (gs[o] > gs[g])
            beaten = beaten + better.astype(jnp.int32)
        keep = beaten < TOPK_GROUPS
        pieces.append(jnp.where(keep, sel[g * gsz:(g + 1) * gsz], neg))
    cur = jnp.concatenate(pieces, axis=0)

    row = lax.broadcasted_iota(jnp.int32, (n_exp, tm), 0)
    idxs, picked = [], []
    for _ in range(TOP_K):
        m = jnp.max(cur, axis=0, keepdims=True)
        idx = jnp.min(jnp.where(cur == m, row, n_exp), axis=0, keepdims=True)
        hit = row == idx
        picked.append(jnp.sum(jnp.where(hit, scores, 0.0), axis=0, keepdims=True))
        cur = jnp.where(hit, neg, cur)
        idxs.append(idx)
    total = picked[0]
    for k in range(1, TOP_K):
        total = total + picked[k]

    chosen = jnp.zeros((n_exp, tm), F32)
    for k in range(TOP_K):
        chosen = chosen + (row == idxs[k]).astype(F32)
    earlier = (lax.broadcasted_iota(jnp.int32, (tm, tm), 0)
               < lax.broadcasted_iota(jnp.int32, (tm, tm), 1)).astype(BF16)
    rank = _bdot(chosen.astype(BF16), earlier) + carry_ref[...]
    for k in range(TOP_K):
        eidx_ref[k:k + 1, :] = idxs[k]
        ew_ref[k:k + 1, :] = picked[k] / total * ROUTED_SCALE
        pos_ref[k:k + 1, :] = jnp.sum(jnp.where(row == idxs[k], rank, 0.0), axis=0,
                                      keepdims=True).astype(jnp.int32)
    carry_ref[...] = carry_ref[...] + jnp.sum(chosen, axis=1, keepdims=True)
    cnt_ref[...] = jnp.broadcast_to(carry_ref[...], cnt_ref.shape).astype(jnp.int32)


def _route(h2_flat, w_router_t_hi, w_router_t_lo, b_router_col):
    n, d = h2_flat.shape
    n_exp = w_router_t_hi.shape[0]
    tm = min(ROUTER_TILE, n)
    const2 = lambda i: (0, 0)
    out_tile = pl.BlockSpec((TOP_K, tm), lambda i: (0, i))
    return pl.pallas_call(
        _router_kernel,
        grid=(n // tm,),
        in_specs=[pl.BlockSpec((tm, d), lambda i: (i, 0)),
                  _resident((n_exp, d), const2),
                  _resident((n_exp, d), const2),
                  _resident((n_exp, 1), const2)],
        out_specs=[out_tile, out_tile, out_tile,
                   pl.BlockSpec((n_exp, V7X_LANES), const2)],
        out_shape=[jax.ShapeDtypeStruct((TOP_K, n), jnp.int32),
                   jax.ShapeDtypeStruct((TOP_K, n), F32),
                   jax.ShapeDtypeStruct((TOP_K, n), jnp.int32),
                   jax.ShapeDtypeStruct((n_exp, V7X_LANES), jnp.int32)],
        scratch_shapes=[pltpu.VMEM((n_exp, 1), F32)],
        compiler_params=pltpu.CompilerParams(
            dimension_semantics=("arbitrary",),
            vmem_limit_bytes=V7X_VMEM_LIMIT_BYTES),
        name="router",
    )(h2_flat, w_router_t_hi, w_router_t_lo, b_router_col)


def _row_gather_copy(src_hbm, row, dst_vmem, slot, sem):
    return pltpu.make_async_copy(src_hbm.at[pl.ds(row, 1)], dst_vmem.at[pl.ds(slot, 1)], sem)


def _expert_kernel(be_ref, nused_ref, tok_ref, h_hbm, wg_ref, wu_ref, wd_ref, y_ref,
                   xbuf, sem):
    del be_ref
    i = pl.program_id(0)
    tb = xbuf.shape[0]

    @pl.when(i < nused_ref[0])
    def _():
        def issue(r, carry):
            _row_gather_copy(h_hbm, tok_ref[0, 0, r], xbuf, r, sem).start()
            return carry

        lax.fori_loop(0, tb, issue, 0, unroll=8)
        pltpu.make_async_copy(h_hbm.at[pl.ds(0, tb)], xbuf, sem).wait()
        x = xbuf[...].astype(BF16)
        gate = _bdot(x, wg_ref[0].astype(BF16))
        up = _bdot(x, wu_ref[0].astype(BF16))
        act = (_silu(gate) * up).astype(BF16)
        y_ref[...] = _bdot(act, wd_ref[0].astype(BF16))


def _routed_experts(h2_flat, row_tok, block_e, n_used, w_gate, w_up, w_down):
    n, d = h2_flat.shape
    n_exp, _, d_e = w_gate.shape
    tb = EXPERT_TILE
    n_blocks = row_tok.shape[0]

    def w_map(i, be, nu):
        return (be[jnp.minimum(i, nu[0] - 1)], 0, 0)

    def row_map(i, be, nu):
        return (jnp.minimum(i, nu[0] - 1), 0, 0)

    grid_spec = pltpu.PrefetchScalarGridSpec(
        num_scalar_prefetch=2,
        grid=(n_blocks,),
        in_specs=[pl.BlockSpec((1, 1, tb), row_map, memory_space=pltpu.SMEM),
                  pl.BlockSpec(memory_space=pl.ANY),
                  pl.BlockSpec((1, d, d_e), w_map),
                  pl.BlockSpec((1, d, d_e), w_map),
                  pl.BlockSpec((1, d_e, d), w_map)],
        out_specs=pl.BlockSpec((tb, d), lambda i, be, nu: (jnp.minimum(i, nu[0] - 1), 0)),
        scratch_shapes=[pltpu.VMEM((tb, d), F32), pltpu.SemaphoreType.DMA(())],
    )
    return pl.pallas_call(
        _expert_kernel,
        grid_spec=grid_spec,
        out_shape=jax.ShapeDtypeStruct((n_blocks * tb, d), F32),
        compiler_params=pltpu.CompilerParams(
            dimension_semantics=("arbitrary",),
            vmem_limit_bytes=V7X_VMEM_LIMIT_BYTES),
        name="routed_experts",
    )(block_e, n_used, row_tok, h2_flat, w_gate, w_up, w_down)


def _combine_kernel(alpha, dest_ref, y_hbm, ew_ref, x1_ref, g2_ref, sc2_ref, sh2_ref,
                    wsg_ref, wsu_ref, wsd_ref, lng_ref, lnb_ref, o_ref, gbuf, sem):
    tm = x1_ref.shape[1]

    def issue(r, carry):
        for k in range(TOP_K):
            _row_gather_copy(y_hbm, dest_ref[0, k, r], gbuf.at[k], r, sem).start()
        return carry

    lax.fori_loop(0, tm, issue, 0, unroll=2)
    x1 = x1_ref[0]
    h = _modulate(x1, sc2_ref[0], sh2_ref[0]).astype(BF16)
    act = (_silu(_bdot(h, wsg_ref[...])) * _bdot(h, wsu_ref[...])).astype(BF16)
    y = _bdot(act, wsd_ref[...])
    for k in range(TOP_K):
        pltpu.make_async_copy(y_hbm.at[pl.ds(0, tm)], gbuf.at[k], sem).wait()
    ew = ew_ref[...]
    for k in range(TOP_K):
        y = y + ew[:, k:k + 1] * gbuf[k]
    o_ref[0] = _layer_norm(alpha * x1 + g2_ref[0] * y, lng_ref[...], lnb_ref[...])


def _combine(alpha, dest_tiles, y_sorted, ew_tok, x1, g2, sc2, sh2, ws_gate, ws_up, ws_down,
             ln_g, ln_b):
    bsz, seq, d = x1.shape
    tm = TOKEN_TILE
    nj = seq // tm
    tile = pl.BlockSpec((1, tm, d), lambda b, j: (b, j, 0))
    vec = pl.BlockSpec((1, 1, d), lambda b, j: (b, 0, 0))
    const2 = lambda b, j: (0, 0)
    return pl.pallas_call(
        functools.partial(_combine_kernel, alpha),
        grid=(bsz, nj),
        in_specs=[pl.BlockSpec((1, TOP_K, tm), lambda b, j: (b * nj + j, 0, 0),
                               memory_space=pltpu.SMEM),
                  pl.BlockSpec(memory_space=pl.ANY),
                  pl.BlockSpec((tm, TOP_K), lambda b, j: (b * nj + j, 0)),
                  tile, vec, vec, vec,
                  _resident(ws_gate.shape, const2),
                  _resident(ws_up.shape, const2),
                  _resident(ws_down.shape, const2),
                  _resident(ln_g.shape, const2),
                  _resident(ln_b.shape, const2)],
        out_specs=tile,
        out_shape=jax.ShapeDtypeStruct((bsz, seq, d), F32),
        scratch_shapes=[pltpu.VMEM((TOP_K, tm, d), F32), pltpu.SemaphoreType.DMA(())],
        compiler_params=pltpu.CompilerParams(
            dimension_semantics=("arbitrary", "arbitrary"),
            vmem_limit_bytes=V7X_VMEM_LIMIT_BYTES),
        name="combine",
    )(dest_tiles, y_sorted, ew_tok, x1, g2, sc2, sh2, ws_gate, ws_up, ws_down, ln_g, ln_b)


def _dispatch_plan(eidx_t, pos_t, counts, n_tokens, tb):
    n_exp = counts.shape[0]
    n_pairs = n_tokens * TOP_K
    n_blocks = -(-(n_pairs + n_exp * (tb - 1)) // tb)
    padded = (counts + tb - 1) // tb * tb
    pends = jnp.cumsum(padded)
    pstarts = pends - padded
    dest = pstarts[eidx_t] + pos_t
    tok = jnp.broadcast_to(jnp.arange(n_tokens, dtype=jnp.int32)[None, :], dest.shape)
    row_tok = jnp.zeros((n_blocks * tb,), jnp.int32).at[dest.reshape(-1)].set(tok.reshape(-1))
    block_start = jnp.arange(n_blocks, dtype=jnp.int32) * tb
    block_e = jnp.minimum(jnp.searchsorted(pends, block_start, side="right"),
                          n_exp - 1).astype(jnp.int32)
    n_used = (pends[-1] // tb).astype(jnp.int32).reshape(1)
    return dest, row_tok.reshape(n_blocks, 1, tb), block_e, n_used


def kernel(x, c, w_ada, b_ada, w_in, w_s, b_s, g_v, b_v, w_pool, b_pool, ls_pool, w_pa, w_pb,
           w_o, ln1_g, ln1_b, w_router, b_router, w_gate, w_up, w_down, ws_gate, ws_up,
           ws_down, ln2_g, ln2_b):
    bsz, seq, d = x.shape
    depth = w_ada.shape[0]
    alpha = (2.0 * depth) ** 0.25
    width = w_pa.shape[1]
    pool_width = w_pb.shape[1]
    n_tokens = bsz * seq
    tm = TOKEN_TILE
    for l in range(depth):
        mod = _adaln_mod(c, w_ada[l], b_ada[l])
        sh1, sc1, g1, sh2, sc2, g2 = [m.reshape(bsz, 1, d) for m in jnp.split(mod, 6, axis=-1)]

        w_in_l = w_in[l].astype(BF16)
        o_p = 2 * width
        o_ga = o_p + pool_width
        o_gb = o_ga + d
        m_a = _gmlp_branch(x, sc1, sh1, w_in_l[:, :o_p], w_in_l[:, o_ga:o_gb],
                           w_pa[l].astype(BF16), w_s[l], b_s[l].T,
                           g_v[l].reshape(1, -1), b_v[l].reshape(1, -1))
        m_b = _pool_branch(x, sc1, sh1, w_in_l[:, o_p:o_ga], w_in_l[:, o_gb:],
                           w_pb[l].astype(BF16), w_pool[l].astype(BF16),
                           b_pool[l].reshape(1, -1), ls_pool[l].reshape(1, -1))
        x1, h2 = _post_mix(alpha, m_a, m_b, x, g1, sc2, sh2, w_o[l].astype(BF16),
                           ln1_g[l].reshape(1, -1), ln1_b[l].reshape(1, -1))

        h2_flat = h2.reshape(n_tokens, d)
        w_r_t = w_router[l].T
        w_r_hi = w_r_t.astype(BF16)
        w_r_lo = (w_r_t - w_r_hi.astype(F32)).astype(BF16)
        eidx_t, ew_t, pos_t, cnt = _route(h2_flat, w_r_hi, w_r_lo, b_router[l].reshape(-1, 1))
        dest, row_tok, block_e, n_used = _dispatch_plan(eidx_t, pos_t, cnt[:, 0], n_tokens,
                                                        EXPERT_TILE)
        y_sorted = _routed_experts(h2_flat, row_tok, block_e, n_used, w_gate[l], w_up[l],
                                   w_down[l])
        dest_tiles = dest.reshape(TOP_K, n_tokens // tm, tm).transpose(1, 0, 2)
        x = _combine(alpha, dest_tiles, y_sorted, ew_t.T, x1, g2, sc2, sh2,
                     ws_gate[l].astype(BF16), ws_up[l].astype(BF16), ws_down[l].astype(BF16),
                     ln2_g[l].reshape(1, -1), ln2_b[l].reshape(1, -1))
    return x
```

```python
import functools
import math

import jax
import jax.numpy as jnp
from jax import lax
from jax.experimental import pallas as pl
from jax.experimental.pallas import tpu as pltpu

F32 = jnp.float32
BF16 = jnp.bfloat16

CHUNK = 64
GMLP_GROUPS = 8
GMLP_BLOCK = 128
POOL_WINDOWS = (2, 4, 8, 16)
TOP_K = 8
N_GROUPS = 8
TOPK_GROUPS = 4
ROUTED_SCALE = 2.5
LN_EPS = 1e-5

V7X_LANES = 128
V7X_SUBLANES = 8
V7X_VMEM_LIMIT_BYTES = 56 * 1024 * 1024

TOKEN_TILE = 256
ROUTER_TILE = 512
EXPERT_TILE = 256
DEST_TILE = 2048
COMBINE_TILE = 128
POOL_HALO = 16


def _resident(shape, index_map):
    return pl.BlockSpec(shape, index_map, pipeline_mode=pl.Buffered(1))


def _modulate(x, scale, shift):
    return x * (1.0 + scale) + shift


def _layer_norm(r, gain, bias):
    mu = jnp.mean(r, axis=-1, keepdims=True)
    d = r - mu
    var = jnp.mean(d * d, axis=-1, keepdims=True)
    return d * lax.rsqrt(var + LN_EPS) * gain + bias


def _gelu(x):
    return 0.5 * x * (1.0 + lax.erf(x * math.sqrt(0.5)))


def _silu(x):
    return x * jax.nn.sigmoid(x)


def _bdot(a, b):
    return jnp.dot(a, b, preferred_element_type=F32)


def _mod_kernel(c_ref, w_ref, b_ref, o_ref):
    s = _silu(c_ref[...])
    o_ref[...] = jnp.dot(s, w_ref[...], precision=lax.Precision.HIGHEST,
                         preferred_element_type=F32) + b_ref[...]


def _adaln_mod(c, w_ada, b_ada):
    bsz, d = c.shape
    n = w_ada.shape[1]
    tn = 1024
    return pl.pallas_call(
        _mod_kernel,
        grid=(n // tn,),
        in_specs=[pl.BlockSpec((bsz, d), lambda j: (0, 0)),
                  pl.BlockSpec((d, tn), lambda j: (0, j)),
                  pl.BlockSpec((1, tn), lambda j: (0, j))],
        out_specs=pl.BlockSpec((bsz, tn), lambda j: (0, j)),
        out_shape=jax.ShapeDtypeStruct((bsz, n), F32),
        name="adaln_mod",
    )(c, w_ada, b_ada.reshape(1, n))


def _gmlp_kernel(x_ref, sc_ref, sh_ref, wuv_ref, wga_ref, wpa_ref, ws_ref, bst_ref,
                 gv_ref, bv_ref, o_ref, ya_ref):
    tm = x_ref.shape[1]
    width = wpa_ref.shape[0]
    gdim = width // GMLP_GROUPS
    h = _modulate(x_ref[0], sc_ref[0], sh_ref[0]).astype(BF16)
    zuv = _bdot(h, wuv_ref[...])
    u = _gelu(zuv[:, :width])
    v = _gelu(zuv[:, width:])
    qi = lax.broadcasted_iota(jnp.int32, (GMLP_BLOCK, GMLP_BLOCK), 0) // CHUNK
    kj = lax.broadcasted_iota(jnp.int32, (GMLP_BLOCK, GMLP_BLOCK), 1) // CHUNK
    visible = kj <= qi
    for g in range(GMLP_GROUPS):
        lo, hi = g * gdim, (g + 1) * gdim
        vn = _layer_norm(v[:, lo:hi], gv_ref[:, lo:hi], bv_ref[:, lo:hi]).astype(BF16)
        wg = jnp.where(visible, ws_ref[g], 0.0).astype(BF16)
        for blk in range(tm // GMLP_BLOCK):
            r0, r1 = blk * GMLP_BLOCK, (blk + 1) * GMLP_BLOCK
            s = _bdot(wg, vn[r0:r1]) + bst_ref[:, g:g + 1]
            ya_ref[r0:r1, lo:hi] = (u[r0:r1, lo:hi] * s).astype(BF16)
    pa = _bdot(ya_ref[...], wpa_ref[...])
    zga = _bdot(h, wga_ref[...])
    o_ref[0] = (jax.nn.sigmoid(zga) * pa).astype(o_ref.dtype)


def _gmlp_branch(x, sc1, sh1, w_uv, w_ga, w_pa, w_s, b_s_t, g_v, b_v):
    bsz, seq, d = x.shape
    tm = TOKEN_TILE
    width = w_pa.shape[0]
    const2 = lambda b, j: (0, 0)
    return pl.pallas_call(
        _gmlp_kernel,
        grid=(bsz, seq // tm),
        in_specs=[pl.BlockSpec((1, tm, d), lambda b, j: (b, j, 0)),
                  pl.BlockSpec((1, 1, d), lambda b, j: (b, 0, 0)),
                  pl.BlockSpec((1, 1, d), lambda b, j: (b, 0, 0)),
                  _resident(w_uv.shape, const2),
                  _resident(w_ga.shape, const2),
                  _resident(w_pa.shape, const2),
                  _resident(w_s.shape, lambda b, j: (0, 0, 0)),
                  _resident(b_s_t.shape, const2),
                  _resident(g_v.shape, const2),
                  _resident(b_v.shape, const2)],
        out_specs=pl.BlockSpec((1, tm, d), lambda b, j: (b, j, 0)),
        out_shape=jax.ShapeDtypeStruct((bsz, seq, d), BF16),
        scratch_shapes=[pltpu.VMEM((tm, width), BF16)],
        compiler_params=pltpu.CompilerParams(
            dimension_semantics=("parallel", "parallel"),
            vmem_limit_bytes=V7X_VMEM_LIMIT_BYTES),
        name="gmlp_branch",
    )(x, sc1, sh1, w_uv, w_ga, w_pa, w_s, b_s_t, g_v, b_v)


def _pool_kernel(x_ref, sc_ref, sh_ref, wp_ref, wgb_ref, wpb_ref, wpool_ref, bpool_ref,
                 ls_ref, o_ref, zbuf, yb_ref):
    j = pl.program_id(1)
    tm = x_ref.shape[1]
    width = wpb_ref.shape[0]
    gdim = width // len(POOL_WINDOWS)
    h = _modulate(x_ref[0], sc_ref[0], sh_ref[0]).astype(BF16)

    @pl.when(j == 0)
    def _():
        zbuf[0:POOL_HALO, :] = jnp.zeros((POOL_HALO, width), F32)

    zbuf[POOL_HALO:POOL_HALO + tm, :] = _bdot(h, wp_ref[...])
    t = lax.broadcasted_iota(jnp.int32, (tm, 1), 0) + j * tm
    for g, win in enumerate(POOL_WINDOWS):
        lo, hi = g * gdim, (g + 1) * gdim
        cur = zbuf[POOL_HALO:POOL_HALO + tm, lo:hi]
        acc = cur
        for k in range(1, win):
            acc = acc + zbuf[POOL_HALO - k:POOL_HALO - k + tm, lo:hi]
        cnt = jnp.minimum(t + 1, win).astype(F32)
        pooled = acc / cnt - cur
        y = _bdot(pooled.astype(BF16), wpool_ref[g]) + bpool_ref[:, lo:hi]
        yb_ref[:, lo:hi] = (y * ls_ref[:, lo:hi]).astype(BF16)
    zbuf[0:POOL_HALO, :] = zbuf[tm:tm + POOL_HALO, :]
    pb = _bdot(yb_ref[...], wpb_ref[...])
    zgb = _bdot(h, wgb_ref[...])
    o_ref[0] = (jax.nn.sigmoid(zgb) * pb).astype(o_ref.dtype)


def _pool_branch(x, sc1, sh1, w_p, w_gb, w_pb, w_pool, b_pool, ls_pool):
    bsz, seq, d = x.shape
    tm = TOKEN_TILE
    width = w_pb.shape[0]
    const2 = lambda b, j: (0, 0)
    return pl.pallas_call(
        _pool_kernel,
        grid=(bsz, seq // tm),
        in_specs=[pl.BlockSpec((1, tm, d), lambda b, j: (b, j, 0)),
                  pl.BlockSpec((1, 1, d), lambda b, j: (b, 0, 0)),
                  pl.BlockSpec((1, 1, d), lambda b, j: (b, 0, 0)),
                  _resident(w_p.shape, const2),
                  _resident(w_gb.shape, const2),
                  _resident(w_pb.shape, const2),
                  _resident(w_pool.shape, lambda b, j: (0, 0, 0)),
                  _resident(b_pool.shape, const2),
                  _resident(ls_pool.shape, const2)],
        out_specs=pl.BlockSpec((1, tm, d), lambda b, j: (b, j, 0)),
        out_shape=jax.ShapeDtypeStruct((bsz, seq, d), BF16),
        scratch_shapes=[pltpu.VMEM((POOL_HALO + tm, width), F32),
                        pltpu.VMEM((tm, width), BF16)],
        compiler_params=pltpu.CompilerParams(
            dimension_semantics=("arbitrary", "arbitrary"),
            vmem_limit_bytes=V7X_VMEM_LIMIT_BYTES),
        name="pool_branch",
    )(x, sc1, sh1, w_p, w_gb, w_pb, w_pool, b_pool, ls_pool)


def _post_mix_kernel(alpha, ma_ref, mb_ref, x_ref, g1_ref, sc2_ref, sh2_ref, wo_ref,
                     lng_ref, lnb_ref, x1_ref, h2_ref):
    m = (ma_ref[0].astype(F32) + mb_ref[0].astype(F32)).astype(BF16)
    mix = _bdot(m, wo_ref[...])
    x1 = _layer_norm(alpha * x_ref[0] + g1_ref[0] * mix, lng_ref[...], lnb_ref[...])
    x1_ref[0] = x1
    h2_ref[0] = _modulate(x1, sc2_ref[0], sh2_ref[0])


def _post_mix(alpha, m_a, m_b, x, g1, sc2, sh2, w_o, ln_g, ln_b):
    bsz, seq, d = x.shape
    tm = TOKEN_TILE
    tile = pl.BlockSpec((1, tm, d), lambda b, j: (b, j, 0))
    vec = pl.BlockSpec((1, 1, d), lambda b, j: (b, 0, 0))
    const2 = lambda b, j: (0, 0)
    return pl.pallas_call(
        functools.partial(_post_mix_kernel, alpha),
        grid=(bsz, seq // tm),
        in_specs=[tile, tile, tile, vec, vec, vec,
                  _resident(w_o.shape, const2),
                  _resident(ln_g.shape, const2),
                  _resident(ln_b.shape, const2)],
        out_specs=[tile, tile],
        out_shape=[jax.ShapeDtypeStruct((bsz, seq, d), F32),
                   jax.ShapeDtypeStruct((bsz, seq, d), F32)],
        compiler_params=pltpu.CompilerParams(
            dimension_semantics=("parallel", "parallel"),
            vmem_limit_bytes=V7X_VMEM_LIMIT_BYTES),
        name="post_mix",
    )(m_a, m_b, x, g1, sc2, sh2, w_o, ln_g, ln_b)


def _router_kernel(h_ref, whi_ref, wlo_ref, br_ref, eidx_ref, ew_ref, pos_ref, cnt_ref,
                   carry_ref):
    i = pl.program_id(0)

    @pl.when(i == 0)
    def _():
        carry_ref[...] = jnp.zeros(carry_ref.shape, F32)

    h = h_ref[...]
    h_hi = h.astype(BF16)
    h_lo = (h - h_hi.astype(F32)).astype(BF16)
    nt = (((1,), (1,)), ((), ()))
    w_hi = whi_ref[...]
    logits = (lax.dot_general(w_hi, h_hi, nt, preferred_element_type=F32)
              + lax.dot_general(w_hi, h_lo, nt, preferred_element_type=F32)
              + lax.dot_general(wlo_ref[...], h_hi, nt, preferred_element_type=F32))
    scores = jax.nn.sigmoid(logits)
    sel = scores + br_ref[...]
    n_exp, tm = sel.shape
    gsz = n_exp // N_GROUPS
    neg = -jnp.inf

    gs = []
    sub = lax.broadcasted_iota(jnp.int32, (gsz, tm), 0)
    for g in range(N_GROUPS):
        blk = sel[g * gsz:(g + 1) * gsz]
        m1 = jnp.max(blk, axis=0, keepdims=True)
        first = jnp.min(jnp.where(blk == m1, sub, gsz), axis=0, keepdims=True)
        m2 = jnp.max(jnp.where(sub == first, neg, blk), axis=0, keepdims=True)
        gs.append(m1 + m2)
    pieces = []
    for g in range(N_GROUPS):
        beaten = jnp.zeros((1, tm), jnp.int32)
        for o in range(N_GROUPS):
            if o == g:
                continue
            better = (gs[o] >= gs[g]) if o < g else (gs[o] > gs[g])
            beaten = beaten + better.astype(jnp.int32)
        keep = beaten < TOPK_GROUPS
        pieces.append(jnp.where(keep, sel[g * gsz:(g + 1) * gsz], neg))
    cur = jnp.concatenate(pieces, axis=0)

    row = lax.broadcasted_iota(jnp.int32, (n_exp, tm), 0)
    idxs, picked = [], []
    for _ in range(TOP_K):
        m = jnp.max(cur, axis=0, keepdims=True)
        idx = jnp.min(jnp.where(cur == m, row, n_exp), axis=0, keepdims=True)
        hit = row == idx
        picked.append(jnp.sum(jnp.where(hit, scores, 0.0), axis=0, keepdims=True))
        cur = jnp.where(hit, neg, cur)
        idxs.append(idx)
    total = picked[0]
    for k in range(1, TOP_K):
        total = total + picked[k]

    chosen = jnp.zeros((n_exp, tm), F32)
    for k in range(TOP_K):
        chosen = chosen + (row == idxs[k]).astype(F32)
    earlier = (lax.broadcasted_iota(jnp.int32, (tm, tm), 0)
               < lax.broadcasted_iota(jnp.int32, (tm, tm), 1)).astype(BF16)
    rank = _bdot(chosen.astype(BF16), earlier) + carry_ref[...]
    for k in range(TOP_K):
        eidx_ref[k:k + 1, :] = idxs[k]
        ew_ref[k:k + 1, :] = picked[k] / total * ROUTED_SCALE
        pos_ref[k:k + 1, :] = jnp.sum(jnp.where(row == idxs[k], rank, 0.0), axis=0,
                                      keepdims=True).astype(jnp.int32)
    carry_ref[...] = carry_ref[...] + jnp.sum(chosen, axis=1, keepdims=True)
    cnt_ref[...] = jnp.broadcast_to(carry_ref[...], cnt_ref.shape).astype(jnp.int32)


def _route(h2_flat, w_router_t_hi, w_router_t_lo, b_router_col):
    n, d = h2_flat.shape
    n_exp = w_router_t_hi.shape[0]
    tm = min(ROUTER_TILE, n)
    const2 = lambda i: (0, 0)
    out_tile = pl.BlockSpec((TOP_K, tm), lambda i: (0, i))
    return pl.pallas_call(
        _router_kernel,
        grid=(n // tm,),
        in_specs=[pl.BlockSpec((tm, d), lambda i: (i, 0)),
                  _resident((n_exp, d), const2),
                  _resident((n_exp, d), const2),
                  _resident((n_exp, 1), const2)],
        out_specs=[out_tile, out_tile, out_tile,
                   pl.BlockSpec((n_exp, V7X_LANES), const2)],
        out_shape=[jax.ShapeDtypeStruct((TOP_K, n), jnp.int32),
                   jax.ShapeDtypeStruct((TOP_K, n), F32),
                   jax.ShapeDtypeStruct((TOP_K, n), jnp.int32),
                   jax.ShapeDtypeStruct((n_exp, V7X_LANES), jnp.int32)],
        scratch_shapes=[pltpu.VMEM((n_exp, 1), F32)],
        compiler_params=pltpu.CompilerParams(
            dimension_semantics=("arbitrary",),
            vmem_limit_bytes=V7X_VMEM_LIMIT_BYTES),
        name="router",
    )(h2_flat, w_router_t_hi, w_router_t_lo, b_router_col)


def _row_gather_copy(src_hbm, row, dst_vmem, slot, sem):
    return pltpu.make_async_copy(src_hbm.at[pl.ds(row, 1)], dst_vmem.at[pl.ds(slot, 1)], sem)


def _expert_kernel(be_ref, nxt_ref, nused_ref, tokc_ref, tokn_ref, h_hbm, wg_hbm, wu_hbm,
                   wd_hbm, y_ref, xbuf, wg_st, wu_st, wd_st, wg_b, wu_b, wd_b, gsem, wsem):
    i = pl.program_id(0)
    n_used = nused_ref[0]
    tb = xbuf.shape[1]
    d_e = wg_b.shape[1]
    d = wd_b.shape[1]

    def weight_copies(e):
        return (pltpu.make_async_copy(wg_hbm.at[e], wg_st, wsem.at[0]),
                pltpu.make_async_copy(wu_hbm.at[e], wu_st, wsem.at[1]),
                pltpu.make_async_copy(wd_hbm.at[e], wd_st, wsem.at[2]))

    @pl.when(i < n_used)
    def _():
        slot = lax.rem(i, 2)
        nslot = 1 - slot
        e = be_ref[i]
        first = i == 0
        changed = jnp.logical_or(first, e != be_ref[jnp.maximum(i - 1, 0)])

        @pl.when(first)
        def _():
            for cp in weight_copies(e):
                cp.start()
            for r in range(tb):
                _row_gather_copy(h_hbm, tokc_ref[0, 0, r], xbuf.at[0], r, gsem.at[0]).start()

        @pl.when(changed)
        def _():
            for cp in weight_copies(e):
                cp.wait()
            wg_b[...] = wg_st[...].astype(BF16)
            wu_b[...] = wu_st[...].astype(BF16)
            wd_b[...] = wd_st[...].astype(BF16)
            nxt = nxt_ref[i]

            @pl.when(nxt >= 0)
            def _():
                for cp in weight_copies(nxt):
                    cp.start()

        pltpu.make_async_copy(h_hbm.at[pl.ds(0, tb)], xbuf.at[slot], gsem.at[slot]).wait()
        x = xbuf[slot].astype(BF16)

        col = 2 * V7X_LANES
        n_pieces = 2 * (d_e // col) + d // col
        per_piece = -(-tb // n_pieces)
        pending = iter(range(tb))

        def request_rows(count):
            for _ in range(count):
                r = next(pending, None)
                if r is None:
                    return
                _row_gather_copy(h_hbm, tokn_ref[0, 0, r], xbuf.at[nslot], r,
                                 gsem.at[nslot]).start()

        acts = []
        for n0 in range(0, d_e, col):
            request_rows(per_piece)
            gate = _bdot(x, wg_b[:, n0:n0 + col])
            request_rows(per_piece)
            up = _bdot(x, wu_b[:, n0:n0 + col])
            acts.append((_silu(gate) * up).astype(BF16))
        act = jnp.concatenate(acts, axis=1)
        for n0 in range(0, d, col):
            request_rows(per_piece)
            y_ref[:, n0:n0 + col] = _bdot(act, wd_b[:, n0:n0 + col])
        request_rows(tb)

        @pl.when(i == n_used - 1)
        def _():
            pltpu.make_async_copy(h_hbm.at[pl.ds(0, tb)], xbuf.at[nslot], gsem.at[nslot]).wait()

    @pl.when(i >= n_used)
    def _():
        y_ref[...] = jnp.zeros(y_ref.shape, y_ref.dtype)


def _routed_experts(h2_flat, row_tok, block_e, next_e, n_used, w_gate, w_up, w_down):
    n, d = h2_flat.shape
    n_exp, _, d_e = w_gate.shape
    tb = EXPERT_TILE
    n_blocks = row_tok.shape[0]

    def cur_map(i, be, nx, nu):
        return (jnp.minimum(i, nu[0] - 1), 0, 0)

    def next_map(i, be, nx, nu):
        return (jnp.minimum(i + 1, nu[0] - 1), 0, 0)

    grid_spec = pltpu.PrefetchScalarGridSpec(
        num_scalar_prefetch=3,
        grid=(n_blocks,),
        in_specs=[pl.BlockSpec((1, 1, tb), cur_map, memory_space=pltpu.SMEM),
                  pl.BlockSpec((1, 1, tb), next_map, memory_space=pltpu.SMEM),
                  pl.BlockSpec(memory_space=pl.ANY),
                  pl.BlockSpec(memory_space=pl.ANY),
                  pl.BlockSpec(memory_space=pl.ANY),
                  pl.BlockSpec(memory_space=pl.ANY)],
        out_specs=pl.BlockSpec((tb, d), lambda i, be, nx, nu: (i, 0)),
        scratch_shapes=[pltpu.VMEM((2, tb, d), F32),
                        pltpu.VMEM((d, d_e), F32), pltpu.VMEM((d, d_e), F32),
                        pltpu.VMEM((d_e, d), F32),
                        pltpu.VMEM((d, d_e), BF16), pltpu.VMEM((d, d_e), BF16),
                        pltpu.VMEM((d_e, d), BF16),
                        pltpu.SemaphoreType.DMA((2,)), pltpu.SemaphoreType.DMA((3,))],
    )
    return pl.pallas_call(
        _expert_kernel,
        grid_spec=grid_spec,
        out_shape=jax.ShapeDtypeStruct((n_blocks * tb, d), F32),
        compiler_params=pltpu.CompilerParams(
            dimension_semantics=("arbitrary",),
            vmem_limit_bytes=V7X_VMEM_LIMIT_BYTES),
        name="routed_experts",
    )(block_e, next_e, n_used, row_tok, row_tok, h2_flat, w_gate, w_up, w_down)


def _combine_kernel(alpha, destc_ref, destn_ref, y_hbm, ew_ref, x1_ref, g2_ref, sc2_ref,
                    sh2_ref, wsg_ref, wsu_ref, wsd_ref, lng_ref, lnb_ref, o_ref, gbuf, sem):
    tm = x1_ref.shape[1]
    d_e = wsg_ref.shape[1]
    d = wsd_ref.shape[1]
    step = pl.program_id(0) * pl.num_programs(1) + pl.program_id(1)
    last = pl.num_programs(0) * pl.num_programs(1) - 1
    slot = lax.rem(step, 2)
    nslot = 1 - slot

    def wait_tile(s):
        for k in range(TOP_K):
            pltpu.make_async_copy(y_hbm.at[pl.ds(0, tm)], gbuf.at[s, k], sem.at[s]).wait()

    @pl.when(step == 0)
    def _():
        for r in range(tm):
            for k in range(TOP_K):
                _row_gather_copy(y_hbm, destc_ref[0, k, r], gbuf.at[0, k], r, sem.at[0]).start()

    wait_tile(slot)
    pending = iter([(r, k) for r in range(tm) for k in range(TOP_K)])

    def request_rows(count):
        for _ in range(count):
            rk = next(pending, None)
            if rk is None:
                return
            r, k = rk
            _row_gather_copy(y_hbm, destn_ref[0, k, r], gbuf.at[nslot, k], r,
                             sem.at[nslot]).start()

    col = 2 * V7X_LANES
    n_pieces = 2 * (d_e // col) + d // col + TOP_K
    per_piece = -(-(tm * TOP_K) // n_pieces)
    x1 = x1_ref[0]
    h = _modulate(x1, sc2_ref[0], sh2_ref[0]).astype(BF16)
    acts = []
    for n0 in range(0, d_e, col):
        request_rows(per_piece)
        gate = _bdot(h, wsg_ref[:, n0:n0 + col])
        request_rows(per_piece)
        up = _bdot(h, wsu_ref[:, n0:n0 + col])
        acts.append((_silu(gate) * up).astype(BF16))
    act = jnp.concatenate(acts, axis=1)
    ys = []
    for n0 in range(0, d, col):
        request_rows(per_piece)
        ys.append(_bdot(act, wsd_ref[:, n0:n0 + col]))
    y = jnp.concatenate(ys, axis=1)
    ew = ew_ref[...]
    for k in range(TOP_K):
        request_rows(per_piece)
        y = y + ew[:, k:k + 1] * gbuf[slot, k]
    request_rows(tm * TOP_K)
    o_ref[0] = _layer_norm(alpha * x1 + g2_ref[0] * y, lng_ref[...], lnb_ref[...])

    @pl.when(step == last)
    def _():
        wait_tile(nslot)


def _combine(alpha, dest_tiles, y_sorted, ew_tok, x1, g2, sc2, sh2, ws_gate, ws_up, ws_down,
             ln_g, ln_b):
    bsz, seq, d = x1.shape
    tm = COMBINE_TILE
    nj = seq // tm
    n_steps = bsz * nj
    tile = pl.BlockSpec((1, tm, d), lambda b, j: (b, j, 0))
    vec = pl.BlockSpec((1, 1, d), lambda b, j: (b, 0, 0))
    const2 = lambda b, j: (0, 0)
    return pl.pallas_call(
        functools.partial(_combine_kernel, alpha),
        grid=(bsz, nj),
        in_specs=[pl.BlockSpec((1, TOP_K, tm), lambda b, j: (b * nj + j, 0, 0),
                               memory_space=pltpu.SMEM),
                  pl.BlockSpec((1, TOP_K, tm),
                               lambda b, j: (jnp.minimum(b * nj + j + 1, n_steps - 1), 0, 0),
                               memory_space=pltpu.SMEM),
                  pl.BlockSpec(memory_space=pl.ANY),
                  pl.BlockSpec((tm, TOP_K), lambda b, j: (b * nj + j, 0)),
                  tile, vec, vec, vec,
                  _resident(ws_gate.shape, const2),
                  _resident(ws_up.shape, const2),
                  _resident(ws_down.shape, const2),
                  _resident(ln_g.shape, const2),
                  _resident(ln_b.shape, const2)],
        out_specs=tile,
        out_shape=jax.ShapeDtypeStruct((bsz, seq, d), F32),
        scratch_shapes=[pltpu.VMEM((2, TOP_K, tm, d), F32), pltpu.SemaphoreType.DMA((2,))],
        compiler_params=pltpu.CompilerParams(
            dimension_semantics=("arbitrary", "arbitrary"),
            vmem_limit_bytes=V7X_VMEM_LIMIT_BYTES),
        name="combine",
    )(dest_tiles, dest_tiles, y_sorted, ew_tok, x1, g2, sc2, sh2, ws_gate, ws_up, ws_down,
      ln_g, ln_b)


def _dest_kernel(eidx_ref, pos_ref, pstart_ref, dest_ref):
    n_exp = pstart_ref.shape[0]
    tm = eidx_ref.shape[1]
    row = lax.broadcasted_iota(jnp.int32, (n_exp, tm), 0)
    pstart = pstart_ref[...]
    for k in range(TOP_K):
        base = jnp.sum(jnp.where(row == eidx_ref[k:k + 1, :], pstart, 0.0), axis=0,
                       keepdims=True)
        dest_ref[k:k + 1, :] = base.astype(jnp.int32) + pos_ref[k:k + 1, :]


def _pair_destinations(eidx_t, pos_t, pstarts):
    n = eidx_t.shape[1]
    n_exp = pstarts.shape[0]
    tm = min(DEST_TILE, n)
    tile = pl.BlockSpec((TOP_K, tm), lambda i: (0, i))
    return pl.pallas_call(
        _dest_kernel,
        grid=(n // tm,),
        in_specs=[tile, tile, _resident((n_exp, 1), lambda i: (0, 0))],
        out_specs=tile,
        out_shape=jax.ShapeDtypeStruct((TOP_K, n), jnp.int32),
        compiler_params=pltpu.CompilerParams(dimension_semantics=("parallel",)),
        name="pair_destinations",
    )(eidx_t, pos_t, pstarts.astype(F32).reshape(n_exp, 1))


def _dispatch_plan(eidx_t, pos_t, counts, n_tokens, tb):
    n_exp = counts.shape[0]
    n_pairs = n_tokens * TOP_K
    n_blocks = -(-(n_pairs + n_exp * (tb - 1)) // tb)
    padded = (counts + tb - 1) // tb * tb
    pends = jnp.cumsum(padded)
    pstarts = pends - padded
    dest = _pair_destinations(eidx_t, pos_t, pstarts)
    tok = jnp.broadcast_to(jnp.arange(n_tokens, dtype=jnp.int32)[None, :], dest.shape)
    row_tok = jnp.zeros((n_blocks * tb,), jnp.int32).at[dest.reshape(-1)].set(
        tok.reshape(-1), unique_indices=True)
    block_start = jnp.arange(n_blocks, dtype=jnp.int32) * tb
    block_e = jnp.minimum(jnp.searchsorted(pends, block_start, side="right"),
                          n_exp - 1).astype(jnp.int32)
    ids = jnp.arange(n_exp, dtype=jnp.int32)
    later = lax.cummin(jnp.where(padded > 0, ids, n_exp), axis=0, reverse=True)
    following = jnp.concatenate([later[1:], jnp.full((1,), n_exp, jnp.int32)])
    following = jnp.where(following >= n_exp, -1, following)
    next_e = following[block_e]
    n_used = (pends[-1] // tb).astype(jnp.int32).reshape(1)
    return dest, row_tok.reshape(n_blocks, 1, tb), block_e, next_e, n_used


def kernel(x, c, w_ada, b_ada, w_in, w_s, b_s, g_v, b_v, w_pool, b_pool, ls_pool, w_pa, w_pb,
           w_o, ln1_g, ln1_b, w_router, b_router, w_gate, w_up, w_down, ws_gate, ws_up,
           ws_down, ln2_g, ln2_b):
    bsz, seq, d = x.shape
    depth = w_ada.shape[0]
    alpha = (2.0 * depth) ** 0.25
    width = w_pa.shape[1]
    pool_width = w_pb.shape[1]
    n_tokens = bsz * seq
    tm = TOKEN_TILE
    for l in range(depth):
        mod = _adaln_mod(c, w_ada[l], b_ada[l])
        sh1, sc1, g1, sh2, sc2, g2 = [m.reshape(bsz, 1, d) for m in jnp.split(mod, 6, axis=-1)]

        w_in_l = w_in[l].astype(BF16)
        o_p = 2 * width
        o_ga = o_p + pool_width
        o_gb = o_ga + d
        m_a = _gmlp_branch(x, sc1, sh1, w_in_l[:, :o_p], w_in_l[:, o_ga:o_gb],
                           w_pa[l].astype(BF16), w_s[l], b_s[l].T,
                           g_v[l].reshape(1, -1), b_v[l].reshape(1, -1))
        m_b = _pool_branch(x, sc1, sh1, w_in_l[:, o_p:o_ga], w_in_l[:, o_gb:],
                           w_pb[l].astype(BF16), w_pool[l].astype(BF16),
                           b_pool[l].reshape(1, -1), ls_pool[l].reshape(1, -1))
        x1, h2 = _post_mix(alpha, m_a, m_b, x, g1, sc2, sh2, w_o[l].astype(BF16),
                           ln1_g[l].reshape(1, -1), ln1_b[l].reshape(1, -1))

        h2_flat = h2.reshape(n_tokens, d)
        w_r_t = w_router[l].T
        w_r_hi = w_r_t.astype(BF16)
        w_r_lo = (w_r_t - w_r_hi.astype(F32)).astype(BF16)
        eidx_t, ew_t, pos_t, cnt = _route(h2_flat, w_r_hi, w_r_lo, b_router[l].reshape(-1, 1))
        dest, row_tok, block_e, next_e, n_used = _dispatch_plan(eidx_t, pos_t, cnt[:, 0],
                                                                n_tokens, EXPERT_TILE)
        y_sorted = _routed_experts(h2_flat, row_tok, block_e, next_e, n_used, w_gate[l],
                                   w_up[l], w_down[l])
        dest_tiles = dest.reshape(TOP_K, n_tokens // COMBINE_TILE, COMBINE_TILE).transpose(1, 0, 2)
        x = _combine(alpha, dest_tiles, y_sorted, ew_t.T, x1, g2, sc2, sh2,
                     ws_gate[l].astype(BF16), ws_up[l].astype(BF16), ws_down[l].astype(BF16),
                     ln2_g[l].reshape(1, -1), ln2_b[l].reshape(1, -1))
    return x
```
